```python
import math
import jax, jax.numpy as jnp
from jax import lax
import numpy as np

D_MODEL = 2048
BATCH = 16
SEQ = 2048
DEPTH = 4
DEC_BATCH = 4
DEC_SEQ = 4096
PAST_LEN = 128

N_HEADS = 8
HEAD_DIM = 64
V_DIM = 2 * HEAD_DIM
QK_WIDTH = N_HEADS * 2 * HEAD_DIM
ATTN_WIDTH = N_HEADS * V_DIM
ROT_DIM = HEAD_DIM // 4
ROPE_THETA = 500000.0
Q_BLOCK = 128
SSM_WIDTH = D_MODEL // 4
SSM_GROUP = 16
SSM_GROUPS = SSM_WIDTH // SSM_GROUP
SSM_STATE = 64
DT_MIN = 0.001
DT_MAX = 0.1
POOL_WIDTH = D_MODEL // 4
POOL_WINDOWS = (2, 4, 8, 16)
POOL_GROUP = POOL_WIDTH // len(POOL_WINDOWS)
MIX_WIDTH = ATTN_WIDTH + SSM_WIDTH + POOL_WIDTH
IN_WIDTH = 2 * QK_WIDTH + ATTN_WIDTH + SSM_WIDTH + POOL_WIDTH
D_FF = 4 * D_MODEL
EPS = 1e-6

kernel_name = "hybrid_bidir_parallel_encoder"


def rmsnorm(x, g):
    xf = x.astype(jnp.float32)
    y = xf * lax.rsqrt(jnp.mean(xf * xf, axis=-1, keepdims=True) + EPS)
    return (y * g.astype(jnp.float32)).astype(x.dtype)


def partial_rope(x):
    L = x.shape[1]
    half = ROT_DIM // 2
    inv = ROPE_THETA ** (-jnp.arange(0, ROT_DIM, 2, dtype=jnp.float32) / ROT_DIM)
    ang = jnp.arange(L, dtype=jnp.float32)[:, None] * inv[None, :]
    cos = jnp.cos(ang)[None, :, None, None, :].astype(x.dtype)
    sin = jnp.sin(ang)[None, :, None, None, :].astype(x.dtype)
    x1 = x[..., :half]
    x2 = x[..., half:ROT_DIM]
    return jnp.concatenate([x1 * cos - x2 * sin, x2 * cos + x1 * sin, x[..., ROT_DIM:]], axis=-1)


def diff_attention(q, k, v, lam, subln_g, lam_init):
    B, L = q.shape[0], q.shape[1]
    nb = L // Q_BLOCK
    qb = jnp.moveaxis(q.reshape(B, nb, Q_BLOCK, N_HEADS, 2, HEAD_DIM), 1, 0)
    scale = HEAD_DIM ** -0.5

    def block(qblk):
        s = jnp.einsum('bqhcd,bkhcd->bhcqk', qblk, k).astype(jnp.float32) * scale
        p = jax.nn.softmax(s, axis=-1)
        w = (p[:, :, 0] - lam * p[:, :, 1]).astype(v.dtype)
        return jnp.einsum('bhqk,bkhd->bqhd', w, v)

    o = lax.map(block, qb)
    o = jnp.moveaxis(o, 0, 1).reshape(B, L, N_HEADS, V_DIM)
    o = rmsnorm(o, subln_g) * (1.0 - lam_init)
    return o.reshape(B, L, ATTN_WIDTH)


def _ssm_combine(e1, e2):
    a1r, a1i, b1r, b1i = e1
    a2r, a2i, b2r, b2i = e2
    return (a2r * a1r - a2i * a1i,
            a2r * a1i + a2i * a1r,
            a2r * b1r - a2i * b1i + b2r,
            a2r * b1i + a2i * b1r + b2i)


def ssm_direction(uf, lam_re, lam_im, log_dt, b_re, b_im, c_re, c_im):
    L = uf.shape[1]
    lr = lam_re.astype(jnp.float32)
    li = lam_im.astype(jnp.float32)
    dt = jnp.exp(log_dt.astype(jnp.float32))[:, None]
    mag = jnp.exp(lr * dt)
    ar = mag * jnp.cos(li * dt)
    ai = mag * jnp.sin(li * dt)
    den = lr * lr + li * li
    nr = ar - 1.0
    cr = (nr * lr + ai * li) / den
    ci = (ai * lr - nr * li) / den
    br = b_re.astype(jnp.float32)
    bi = b_im.astype(jnp.float32)
    bbr = cr[..., None] * br - ci[..., None] * bi
    bbi = cr[..., None] * bi + ci[..., None] * br
    xr = jnp.einsum('blgh,gph->blgp', uf, bbr)
    xi = jnp.einsum('blgh,gph->blgp', uf, bbi)
    shp = (1, L) + ar.shape
    a_r = jnp.broadcast_to(ar[None, None], shp)
    a_i = jnp.broadcast_to(ai[None, None], shp)
    _, _, sr, si = lax.associative_scan(_ssm_combine, (a_r, a_i, xr, xi), axis=1)
    return (jnp.einsum('blgp,ghp->blgh', sr, c_re.astype(jnp.float32))
            - jnp.einsum('blgp,ghp->blgh', si, c_im.astype(jnp.float32)))


def s5_mixer(u, lam_re, lam_im, log_dt, b_re, b_im, c_re, c_im, d, glu_w, glu_b):
    B, L, _ = u.shape
    uf = u.astype(jnp.float32).reshape(B, L, SSM_GROUPS, SSM_GROUP)
    y_f = ssm_direction(uf, lam_re[0], lam_im[0], log_dt[0], b_re[0], b_im[0], c_re[0], c_im[0])
    y_b = jnp.flip(ssm_direction(jnp.flip(uf, axis=1), lam_re[1], lam_im[1], log_dt[1],
                                 b_re[1], b_im[1], c_re[1], c_im[1]), axis=1)
    y = (y_f + y_b).reshape(B, L, SSM_WIDTH) + d.astype(jnp.float32) * uf.reshape(B, L, SSM_WIDTH)
    z = jax.nn.gelu(y)
    out = z * jax.nn.sigmoid(z @ glu_w.astype(jnp.float32) + glu_b.astype(jnp.float32))
    return out.astype(u.dtype)


def pool_mixer(u, pool_w, pool_scale):
    B, L, _ = u.shape
    uf = u.astype(jnp.float32).reshape(B, L, len(POOL_WINDOWS), POOL_GROUP)
    cs = jnp.concatenate([jnp.zeros((B, 1) + uf.shape[2:], jnp.float32),
                          jnp.cumsum(uf, axis=1)], axis=1)
    t = jnp.arange(L)
    means = []
    for gi, w in enumerate(POOL_WINDOWS):
        lo, hi = w // 2, w // 2 - 1
        end = jnp.clip(t + hi + 1, 0, L)
        start = jnp.clip(t - lo, 0, L)
        s = cs[:, end, gi] - cs[:, start, gi]
        cnt = (end - start).astype(jnp.float32)
        means.append(s / cnt[None, :, None])
    pooled = jnp.stack(means, axis=2) - uf
    out = jnp.einsum('blgc,gcd->blgd', pooled, pool_w.astype(jnp.float32))
    out = out.reshape(B, L, POOL_WIDTH) * pool_scale.astype(jnp.float32)
    return out.astype(u.dtype)


def trunk(x, g_mix, w_in, lam_q1, lam_k1, lam_q2, lam_k2, attn_subln,
          ssm_lam_re, ssm_lam_im, ssm_log_dt, ssm_b_re, ssm_b_im, ssm_c_re, ssm_c_im,
          ssm_d, glu_w, glu_b, pool_w, pool_scale, w_out, g_mlp, w_up, w_down, g_final):
    B, L, _ = x.shape
    for l in range(DEPTH):
        h = rmsnorm(x, g_mix[l])
        proj = h @ w_in[l]
        o0 = QK_WIDTH
        o1 = o0 + QK_WIDTH
        o2 = o1 + ATTN_WIDTH
        o3 = o2 + SSM_WIDTH
        q = partial_rope(proj[..., :o0].reshape(B, L, N_HEADS, 2, HEAD_DIM))
        k = partial_rope(proj[..., o0:o1].reshape(B, L, N_HEADS, 2, HEAD_DIM))
        v = proj[..., o1:o2].reshape(B, L, N_HEADS, V_DIM)
        u_s = proj[..., o2:o3]
        u_p = proj[..., o3:]
        lam_init = 0.8 - 0.6 * math.exp(-0.3 * l)
        lam = (jnp.exp(jnp.sum(lam_q1[l].astype(jnp.float32) * lam_k1[l].astype(jnp.float32)))
               - jnp.exp(jnp.sum(lam_q2[l].astype(jnp.float32) * lam_k2[l].astype(jnp.float32)))
               + lam_init)
        a_out = diff_attention(q, k, v, lam, attn_subln[l], lam_init)
        s_out = s5_mixer(u_s, ssm_lam_re[l], ssm_lam_im[l], ssm_log_dt[l], ssm_b_re[l], ssm_b_im[l],
                         ssm_c_re[l], ssm_c_im[l], ssm_d[l], glu_w[l], glu_b[l])
        p_out = pool_mixer(u_p, pool_w[l], pool_scale[l])
        mixed = jnp.concatenate([a_out, s_out, p_out], axis=-1)
        x = x + mixed @ w_out[l]
        h2 = rmsnorm(x, g_mlp[l])
        x = x + jnp.square(jax.nn.relu(h2 @ w_up[l])) @ w_down[l]
    return rmsnorm(x, g_final)


def setup_inputs(seed: int = 0) -> dict:
    key = jax.random.key(seed)
    ks = jax.random.split(key, 26)
    f32 = jnp.float32

    def nrm(k, shape, scale):
        return jax.random.normal(k, shape, f32) * scale

    G, P, Hc = SSM_GROUPS, SSM_STATE, SSM_GROUP
    n = jnp.arange(P, dtype=f32)
    return {
        "x_prompt": nrm(ks[0], (BATCH, SEQ, D_MODEL), 1.0),
        "x_sample": nrm(ks[1], (DEC_BATCH, DEC_SEQ, D_MODEL), 1.0),
        "g_mix": 1.0 + nrm(ks[2], (DEPTH, D_MODEL), 0.01),
        "w_in": nrm(ks[3], (DEPTH, D_MODEL, IN_WIDTH), D_MODEL ** -0.5),
        "lam_q1": nrm(ks[4], (DEPTH, HEAD_DIM), 0.1),
        "lam_k1": nrm(ks[5], (DEPTH, HEAD_DIM), 0.1),
        "lam_q2": nrm(ks[6], (DEPTH, HEAD_DIM), 0.1),
        "lam_k2": nrm(ks[7], (DEPTH, HEAD_DIM), 0.1),
        "attn_subln": 1.0 + nrm(ks[8], (DEPTH, V_DIM), 0.01),
        "ssm_lam_re": -0.5 + nrm(ks[9], (DEPTH, 2, G, P), 0.01),
        "ssm_lam_im": math.pi * n + nrm(ks[10], (DEPTH, 2, G, P), 0.01),
        "ssm_log_dt": jax.random.uniform(ks[11], (DEPTH, 2, G), f32,
                                         math.log(DT_MIN), math.log(DT_MAX)),
        "ssm_b_re": nrm(ks[12], (DEPTH, 2, G, P, Hc), (2 * Hc) ** -0.5),
        "ssm_b_im": nrm(ks[13], (DEPTH, 2, G, P, Hc), (2 * Hc) ** -0.5),
        "ssm_c_re": nrm(ks[14], (DEPTH, 2, G, Hc, P), P ** -0.5),
        "ssm_c_im": nrm(ks[15], (DEPTH, 2, G, Hc, P), P ** -0.5),
        "ssm_d": nrm(ks[16], (DEPTH, SSM_WIDTH), 1.0),
        "glu_w": nrm(ks[17], (DEPTH, SSM_WIDTH, SSM_WIDTH), SSM_WIDTH ** -0.5),
        "glu_b": nrm(ks[18], (DEPTH, SSM_WIDTH), 0.01),
        "pool_w": nrm(ks[19], (DEPTH, len(POOL_WINDOWS), POOL_GROUP, POOL_GROUP), POOL_GROUP ** -0.5),
        "pool_scale": 1.0 + nrm(ks[20], (DEPTH, POOL_WIDTH), 0.02),
        "w_out": nrm(ks[21], (DEPTH, MIX_WIDTH, D_MODEL), MIX_WIDTH ** -0.5),
        "g_mlp": 1.0 + nrm(ks[22], (DEPTH, D_MODEL), 0.01),
        "w_up": nrm(ks[23], (DEPTH, D_MODEL, D_FF), D_MODEL ** -0.5),
        "w_down": nrm(ks[24], (DEPTH, D_FF, D_MODEL), D_FF ** -0.5),
        "g_final": 1.0 + nrm(ks[25], (D_MODEL,), 0.01),
    }


def reference(x_prompt, x_sample, g_mix, w_in, lam_q1, lam_k1, lam_q2, lam_k2, attn_subln,
              ssm_lam_re, ssm_lam_im, ssm_log_dt, ssm_b_re, ssm_b_im, ssm_c_re, ssm_c_im,
              ssm_d, glu_w, glu_b, pool_w, pool_scale, w_out, g_mlp, w_up, w_down, g_final):
    y_prompt = trunk(x_prompt, g_mix, w_in, lam_q1, lam_k1, lam_q2, lam_k2, attn_subln,
                     ssm_lam_re, ssm_lam_im, ssm_log_dt, ssm_b_re, ssm_b_im, ssm_c_re, ssm_c_im,
                     ssm_d, glu_w, glu_b, pool_w, pool_scale, w_out, g_mlp, w_up, w_down, g_final)
    y_sample = trunk(x_sample, g_mix, w_in, lam_q1, lam_k1, lam_q2, lam_k2, attn_subln,
                     ssm_lam_re, ssm_lam_im, ssm_log_dt, ssm_b_re, ssm_b_im, ssm_c_re, ssm_c_im,
                     ssm_d, glu_w, glu_b, pool_w, pool_scale, w_out, g_mlp, w_up, w_down, g_final)
    return (y_prompt, y_sample)
```

```python
import functools
import math

import jax
import jax.numpy as jnp
from jax import lax
from jax.experimental import pallas as pl
from jax.experimental.pallas import tpu as pltpu

N_HEADS = 8
HEAD_DIM = 64
V_DIM = 2 * HEAD_DIM
ROT_DIM = HEAD_DIM // 4
ROPE_THETA = 500000.0
SSM_GROUP = 16
SSM_STATE = 64
POOL_WINDOWS = (2, 4, 8, 16)
EPS = 1e-6

LANES = 128
SUBLANES = 8
SSM_TILE_ROWS = 1024
VMEM_LIMIT_BYTES = 56 * 2 ** 20
POOL_HALO = 128

F32 = jnp.float32
BF16 = jnp.bfloat16


def _params(*sem):
    return pltpu.CompilerParams(dimension_semantics=sem, vmem_limit_bytes=VMEM_LIMIT_BYTES)


def _rms(x, g):
    ms = jnp.mean(x * x, axis=-1, keepdims=True)
    return x * lax.rsqrt(ms + EPS) * g


def _inproj_kernel(x_ref, g_ref, w_ref, ra_ref, rb_ref, rc_ref, us_init_ref,
                   q_ref, k_ref, v_ref, us_ref, up_ref, *, qk_w, attn_w, ssm_w):
    del us_init_ref
    h = _rms(x_ref[...], g_ref[...]).astype(BF16)
    ra, rb, rc = ra_ref[...], rb_ref[...], rc_ref[...]

    def rope_heads(col0, out_ref, scale):
        p = jnp.dot(h, w_ref[:, col0:col0 + qk_w], preferred_element_type=F32)
        for hh in range(N_HEADS):
            blk = p[:, hh * V_DIM:(hh + 1) * V_DIM]
            r = (blk * ra + pltpu.roll(blk, LANES - ROT_DIM // 2, 1) * rb
                 + pltpu.roll(blk, ROT_DIM // 2, 1) * rc)
            out_ref[hh] = (r * scale).astype(BF16)

    rope_heads(0, q_ref, HEAD_DIM ** -0.5)
    rope_heads(qk_w, k_ref, 1.0)
    o1 = 2 * qk_w
    pv = jnp.dot(h, w_ref[:, o1:o1 + attn_w], preferred_element_type=F32)
    for hh in range(N_HEADS):
        v_ref[hh] = pv[:, hh * V_DIM:(hh + 1) * V_DIM].astype(BF16)
    o2 = o1 + attn_w
    us_ref[...] = jnp.dot(h, w_ref[:, o2:o2 + ssm_w], preferred_element_type=F32)
    o3 = o2 + ssm_w
    up_ref[...] = jnp.dot(h, w_ref[:, o3:], preferred_element_type=F32).astype(BF16)


def _inproj(x, g, w_in, layer, rope, us_init, *, tm):
    B, L, D = x.shape
    in_w = w_in.shape[-1]
    qk_w = N_HEADS * 2 * HEAD_DIM
    attn_w = N_HEADS * V_DIM
    ssm_w = D // 4
    pool_w = in_w - 2 * qk_w - attn_w - ssm_w
    head_shape = jax.ShapeDtypeStruct((B, N_HEADS, L, V_DIM), BF16)
    head_spec = pl.BlockSpec((None, N_HEADS, tm, V_DIM), lambda b, i: (b, 0, i, 0))
    rope_spec = pl.BlockSpec((tm, LANES), lambda b, i: (i, 0))
    return pl.pallas_call(
        functools.partial(_inproj_kernel, qk_w=qk_w, attn_w=attn_w, ssm_w=ssm_w),
        grid=(B, L // tm),
        in_specs=[
            pl.BlockSpec((None, tm, D), lambda b, i: (b, i, 0)),
            pl.BlockSpec((None, 1, D), lambda b, i: (layer, 0, 0)),
            pl.BlockSpec((None, D, in_w), lambda b, i: (layer, 0, 0),
                         pipeline_mode=pl.Buffered(1)),
            rope_spec, rope_spec, rope_spec,
            pl.BlockSpec(memory_space=pl.ANY),
        ],
        out_specs=[
            head_spec, head_spec, head_spec,
            pl.BlockSpec((tm, ssm_w), lambda b, i: (i, b)),
            pl.BlockSpec((None, tm, pool_w), lambda b, i: (b, i, 0)),
        ],
        out_shape=[
            head_shape, head_shape, head_shape,
            jax.ShapeDtypeStruct(us_init.shape, F32),
            jax.ShapeDtypeStruct((B, L, pool_w), BF16),
        ],
        input_output_aliases={6: 3},
        compiler_params=_params("parallel", "arbitrary"),
        name="inproj",
    )(x, g, w_in, *rope, us_init)


def _attn_kernel(q_ref, k_ref, v_ref, lq1_ref, lk1_ref, lq2_ref, lk2_ref, g_ref, o_ref,
                 *, seq, tq, kc, lam_init):
    lane = lax.broadcasted_iota(jnp.int32, (1, V_DIM), 1)
    first = lane < HEAD_DIM
    lam = (jnp.exp(jnp.sum(lq1_ref[...] * lk1_ref[...], axis=-1, keepdims=True))
           - jnp.exp(jnp.sum(lq2_ref[...] * lk2_ref[...], axis=-1, keepdims=True))
           + lam_init)
    g = g_ref[...]

    def q_body(i, carry):
        r0 = pl.multiple_of(i * tq, tq)
        q = q_ref[pl.ds(r0, tq), :]
        zero = jnp.zeros_like(q)
        q1 = jnp.where(first, q, zero)
        q2 = jnp.where(first, zero, q)

        def kv_body(j, st):
            c0 = pl.multiple_of(j * kc, kc)
            kj = k_ref[pl.ds(c0, kc), :]
            vj = v_ref[pl.ds(c0, kc), :]

            def update(qc, m, l, acc):
                s = lax.dot_general(qc, kj, (((1,), (1,)), ((), ())),
                                    preferred_element_type=F32)
                mn = jnp.maximum(m, jnp.max(s, axis=-1, keepdims=True))
                al = jnp.exp(m - mn)
                e = jnp.exp(s - mn)
                l = al * l + jnp.sum(e, axis=-1, keepdims=True)
                acc = al * acc + jnp.dot(e.astype(BF16), vj, preferred_element_type=F32)
                return mn, l, acc

            m1, l1, a1, m2, l2, a2 = st
            m1, l1, a1 = update(q1, m1, l1, a1)
            m2, l2, a2 = update(q2, m2, l2, a2)
            return m1, l1, a1, m2, l2, a2

        m0 = jnp.full((tq, 1), -1e30, F32)
        l0 = jnp.zeros((tq, 1), F32)
        a0 = jnp.zeros((tq, V_DIM), F32)
        _, l1, a1, _, l2, a2 = lax.fori_loop(0, seq // kc, kv_body, (m0, l0, a0, m0, l0, a0))
        o = a1 / l1 - lam * (a2 / l2)
        o_ref[pl.ds(r0, tq), :] = (_rms(o, g) * (1.0 - lam_init)).astype(o_ref.dtype)
        return carry

    lax.fori_loop(0, seq // tq, q_body, 0)


def _attention(q, k, v, lq1, lk1, lq2, lk2, subln, layer, lam_init, *, tq, kc):
    B, H, L, _ = q.shape
    head_spec = pl.BlockSpec((None, None, L, V_DIM), lambda b, h: (b, h, 0, 0))
    lam_spec = pl.BlockSpec((None, 1, HEAD_DIM), lambda b, h: (layer, 0, 0))
    return pl.pallas_call(
        functools.partial(_attn_kernel, seq=L, tq=tq, kc=kc, lam_init=lam_init),
        grid=(B, H),
        in_specs=[head_spec, head_spec, head_spec, lam_spec, lam_spec, lam_spec, lam_spec,
                  pl.BlockSpec((None, 1, V_DIM), lambda b, h: (layer, 0, 0))],
        out_specs=pl.BlockSpec((None, L, V_DIM), lambda b, h: (b, 0, h)),
        out_shape=jax.ShapeDtypeStruct((B, L, H * V_DIM), BF16),
        compiler_params=_params("parallel", "arbitrary"),
        name="diff_attention",
    )(q, k, v, lq1, lk1, lq2, lk2, subln)


def _ssm_kernel(*refs, batch, tt, reverse, final, nblk, blk_state):
    if final:
        (u_ref, bb_ref, cc_ref, ar_ref, ai_ref, yf_ref, d_ref, gw_ref, gb_ref,
         y_ref, vs_ref, hr_ref, hi_ref, z_ref) = refs
    else:
        u_ref, bb_ref, cc_ref, ar_ref, ai_ref, y_ref, vs_ref, hr_ref, hi_ref = refs

    @pl.when(pl.program_id(0) == 0)
    def _():
        hr_ref[...] = jnp.zeros_like(hr_ref)
        hi_ref[...] = jnp.zeros_like(hi_ref)

    for qb in range(nblk):
        cols = slice(qb * LANES, (qb + 1) * LANES)
        scols = slice(qb * blk_state, (qb + 1) * blk_state)
        ub = u_ref[:, cols].astype(BF16)
        vs_ref[...] = jnp.dot(ub, bb_ref[qb], preferred_element_type=F32)
        ar = jnp.broadcast_to(ar_ref[:, scols], (batch, blk_state))
        ai = jnp.broadcast_to(ai_ref[:, scols], (batch, blk_state))

        def step(s, st):
            hr, hi = st
            t = (tt - 1 - s) if reverse else s
            r0 = pl.multiple_of(t * batch, batch)
            vr = vs_ref[pl.ds(r0, batch), 0:blk_state]
            vi = vs_ref[pl.ds(r0, batch), blk_state:2 * blk_state]
            nr = ar * hr - ai * hi + vr
            ni = ar * hi + ai * hr + vi
            vs_ref[pl.ds(r0, batch), 0:blk_state] = nr
            vs_ref[pl.ds(r0, batch), blk_state:2 * blk_state] = ni
            return nr, ni

        hr, hi = lax.fori_loop(0, tt, step, (hr_ref[qb], hi_ref[qb]), unroll=min(tt, 8))
        hr_ref[qb] = hr
        hi_ref[qb] = hi
        yq = jnp.dot(vs_ref[...].astype(BF16), cc_ref[qb], preferred_element_type=F32)
        if final:
            yt = yq + yf_ref[:, cols] + d_ref[:, cols] * u_ref[:, cols]
            z_ref[:, cols] = jax.nn.gelu(yt)
        else:
            y_ref[:, cols] = yq

    if final:
        z = z_ref[...]
        gate = jax.nn.sigmoid(jnp.dot(z.astype(BF16), gw_ref[...], preferred_element_type=F32)
                              + gb_ref[...])
        y_ref[...] = (z * gate).astype(y_ref.dtype)


def _ssm_pass(u2d, prep, layer, direction, batch, *, tt, yf=None, gate=None):
    rows, width = u2d.shape
    bb, cc, ar, ai = prep
    nblk = width // LANES
    blk_state = (LANES // SSM_GROUP) * SSM_STATE
    tile = tt * batch
    nt = rows // tile
    reverse = direction == 1
    final = yf is not None
    tile_idx = (lambda j: (nt - 1 - j, 0)) if reverse else (lambda j: (j, 0))
    row_spec = pl.BlockSpec((tile, width), tile_idx)
    dir_idx = lambda j: (layer, direction, 0, 0, 0)
    coef_spec = pl.BlockSpec((None, None, 1, nblk * blk_state), lambda j: (layer, direction, 0, 0))
    in_specs = [
        row_spec,
        pl.BlockSpec((None, None, nblk, LANES, 2 * blk_state), dir_idx),
        pl.BlockSpec((None, None, nblk, 2 * blk_state, LANES), dir_idx),
        coef_spec, coef_spec,
    ]
    args = [u2d, bb, cc, ar, ai]
    scratch = [pltpu.VMEM((tile, 2 * blk_state), F32),
               pltpu.VMEM((nblk, batch, blk_state), F32),
               pltpu.VMEM((nblk, batch, blk_state), F32)]
    if final:
        d, gw, gb = gate
        vec_spec = pl.BlockSpec((None, 1, width), lambda j: (layer, 0, 0))
        in_specs += [row_spec, vec_spec,
                     pl.BlockSpec((None, width, width), lambda j: (layer, 0, 0)), vec_spec]
        args += [yf, d, gw, gb]
        scratch.append(pltpu.VMEM((tile, width), F32))
    return pl.pallas_call(
        functools.partial(_ssm_kernel, batch=batch, tt=tt, reverse=reverse, final=final,
                          nblk=nblk, blk_state=blk_state),
        grid=(nt,),
        in_specs=in_specs,
        out_specs=row_spec,
        out_shape=jax.ShapeDtypeStruct((rows, width), BF16 if final else F32),
        scratch_shapes=scratch,
        compiler_params=_params("arbitrary"),
        name="ssm_bwd_gate" if final else "ssm_fwd",
    )(*args)


def _ssm_prepare(lam_re, lam_im, log_dt, b_re, b_im, c_re, c_im):
    depth, _, G, P = lam_re.shape
    Hc = b_re.shape[-1]
    gpb = LANES // Hc
    nblk = G // gpb
    lr = lam_re.astype(F32)
    li = lam_im.astype(F32)
    dt = jnp.exp(log_dt.astype(F32))[..., None]
    mag = jnp.exp(lr * dt)
    ar = mag * jnp.cos(li * dt)
    ai = mag * jnp.sin(li * dt)
    den = lr * lr + li * li
    nr = ar - 1.0
    cr = (nr * lr + ai * li) / den
    ci = (ai * lr - nr * li) / den
    br = b_re.astype(F32)
    bi = b_im.astype(F32)
    bbr = cr[..., None] * br - ci[..., None] * bi
    bbi = cr[..., None] * bi + ci[..., None] * br
    eye = jnp.eye(gpb, dtype=F32)

    def pack_b(w):
        w = w.reshape(depth, 2, nblk, gpb, P, Hc)
        return jnp.einsum('dzqgpj,gh->dzqgjhp', w, eye).reshape(depth, 2, nblk, gpb * Hc, gpb * P)

    def pack_c(w):
        w = w.reshape(depth, 2, nblk, gpb, Hc, P)
        return jnp.einsum('dzqgip,gh->dzqgphi', w, eye).reshape(depth, 2, nblk, gpb * P, gpb * Hc)

    bb = jnp.concatenate([pack_b(bbr), pack_b(bbi)], axis=-1).astype(BF16)
    cc = jnp.concatenate([pack_c(c_re.astype(F32)), -pack_c(c_im.astype(F32))], axis=-2).astype(BF16)
    return bb, cc, ar.reshape(depth, 2, 1, G * P), ai.reshape(depth, 2, 1, G * P)


def _pool_kernel(u_ref, w_ref, sc_ref, o_ref, pad_ref, *, seq, tt):
    nwin = len(POOL_WINDOWS)
    cw = u_ref.shape[-1] // nwin
    zeros = jnp.zeros((POOL_HALO, u_ref.shape[-1]), pad_ref.dtype)
    pad_ref[0:POOL_HALO, :] = zeros
    pad_ref[POOL_HALO + seq:POOL_HALO + seq + POOL_HALO, :] = zeros
    pad_ref[POOL_HALO:POOL_HALO + seq, :] = u_ref[...]

    kw = tt + 2 * POOL_HALO
    row = lax.broadcasted_iota(jnp.int32, (tt, kw), 0)
    col = lax.broadcasted_iota(jnp.int32, (tt, kw), 1)
    off = col - POOL_HALO - row
    bands = [jnp.where((off >= -(w // 2)) & (off <= w // 2 - 1), 1.0, 0.0).astype(BF16)
             for w in POOL_WINDOWS]
    trow = lax.broadcasted_iota(jnp.int32, (tt, 1), 0)

    def tile_body(i, carry):
        t0 = pl.multiple_of(i * tt, tt)
        t = trow + t0
        win = pad_ref[pl.ds(t0, kw), :]
        cur = pad_ref[pl.ds(t0 + POOL_HALO, tt), :].astype(F32)
        for gi, w in enumerate(POOL_WINDOWS):
            cs = slice(gi * cw, (gi + 1) * cw)
            end = jnp.minimum(t + w // 2, seq)
            start = jnp.maximum(t - w // 2, 0)
            cnt = (end - start).astype(F32)
            ssum = jnp.dot(bands[gi], win[:, cs], preferred_element_type=F32)
            pooled = ssum / cnt - cur[:, cs]
            out = jnp.dot(pooled.astype(BF16), w_ref[gi], preferred_element_type=F32)
            o_ref[pl.ds(t0, tt), cs] = (out * sc_ref[:, cs]).astype(o_ref.dtype)
        return carry

    lax.fori_loop(0, seq // tt, tile_body, 0)


def _pool(u, pool_w, pool_scale, layer, *, tt):
    B, L, W = u.shape
    nwin = len(POOL_WINDOWS)
    cw = W // nwin
    seq_spec = pl.BlockSpec((None, L, W), lambda b: (b, 0, 0))
    return pl.pallas_call(
        functools.partial(_pool_kernel, seq=L, tt=tt),
        grid=(B,),
        in_specs=[seq_spec,
                  pl.BlockSpec((None, nwin, cw, cw), lambda b: (layer, 0, 0, 0)),
                  pl.BlockSpec((None, 1, W), lambda b: (layer, 0, 0))],
        out_specs=seq_spec,
        out_shape=jax.ShapeDtypeStruct((B, L, W), BF16),
        scratch_shapes=[pltpu.VMEM((L + 2 * POOL_HALO, W), BF16)],
        compiler_params=_params("parallel"),
        name="pool_mixer",
    )(u, pool_w, pool_scale)


def _outproj_kernel(a_ref, s_ref, p_ref, x_ref, w_ref, o_ref):
    aw = a_ref.shape[-1]
    sw = s_ref.shape[-1]
    acc = jnp.dot(a_ref[...], w_ref[0:aw, :], preferred_element_type=F32)
    acc += jnp.dot(s_ref[...], w_ref[aw:aw + sw, :], preferred_element_type=F32)
    acc += jnp.dot(p_ref[...], w_ref[aw + sw:, :], preferred_element_type=F32)
    o_ref[...] = x_ref[...] + acc


def _outproj(a, s2d, p, x, w_out, layer, *, tm):
    B, L, D = x.shape
    aw, pw = a.shape[-1], p.shape[-1]
    sw = w_out.shape[1] - aw - pw
    return pl.pallas_call(
        _outproj_kernel,
        grid=(B, L // tm),
        in_specs=[
            pl.BlockSpec((None, tm, aw), lambda b, i: (b, i, 0)),
            pl.BlockSpec((tm, sw), lambda b, i: (i, b)),
            pl.BlockSpec((None, tm, pw), lambda b, i: (b, i, 0)),
            pl.BlockSpec((None, tm, D), lambda b, i: (b, i, 0)),
            pl.BlockSpec((None, aw + sw + pw, D), lambda b, i: (layer, 0, 0),
                         pipeline_mode=pl.Buffered(1)),
        ],
        out_specs=pl.BlockSpec((None, tm, D), lambda b, i: (b, i, 0)),
        out_shape=jax.ShapeDtypeStruct((B, L, D), F32),
        input_output_aliases={3: 0},
        compiler_params=_params("parallel", "arbitrary"),
        name="outproj_residual",
    )(a, s2d, p, x, w_out)


def _mlp_kernel(x_ref, g_ref, wu_ref, wd_ref, o_ref, h_ref):
    @pl.when(pl.program_id(1) == 0)
    def _():
        xv = x_ref[...]
        h_ref[...] = _rms(xv, g_ref[...]).astype(BF16)
        o_ref[...] = xv

    u = jnp.dot(h_ref[...], wu_ref[...], preferred_element_type=F32)
    r = jnp.maximum(u, 0.0)
    o_ref[...] += jnp.dot((r * r).astype(BF16), wd_ref[...], preferred_element_type=F32)


def _mlp(x2d, g, w_up, w_down, layer, *, tm, tf):
    T, D = x2d.shape
    F = w_up.shape[-1]
    return pl.pallas_call(
        _mlp_kernel,
        grid=(T // tm, F // tf),
        in_specs=[
            pl.BlockSpec((tm, D), lambda i, j: (i, 0)),
            pl.BlockSpec((None, 1, D), lambda i, j: (layer, 0, 0)),
            pl.BlockSpec((None, D, tf), lambda i, j: (layer, 0, j)),
            pl.BlockSpec((None, tf, D), lambda i, j: (layer, j, 0)),
        ],
        out_specs=pl.BlockSpec((tm, D), lambda i, j: (i, 0)),
        out_shape=jax.ShapeDtypeStruct((T, D), F32),
        scratch_shapes=[pltpu.VMEM((tm, D), BF16)],
        input_output_aliases={0: 0},
        compiler_params=_params("parallel", "arbitrary"),
        name="mlp_residual",
    )(x2d, g, w_up, w_down)


def _final_norm_kernel(x_ref, g_ref, o_ref):
    o_ref[...] = _rms(x_ref[...], g_ref[...])


def _final_norm(x2d, g, *, tm):
    T, D = x2d.shape
    return pl.pallas_call(
        _final_norm_kernel,
        grid=(T // tm,),
        in_specs=[pl.BlockSpec((tm, D), lambda i: (i, 0)), pl.BlockSpec((1, D), lambda i: (0, 0))],
        out_specs=pl.BlockSpec((tm, D), lambda i: (i, 0)),
        out_shape=jax.ShapeDtypeStruct((T, D), F32),
        compiler_params=_params("parallel"),
        name="final_norm",
    )(x2d, g)


def _rope_tables(seq):
    half = ROT_DIM // 2
    inv = ROPE_THETA ** (-jnp.arange(0, ROT_DIM, 2, dtype=F32) / ROT_DIM)
    ang = jnp.arange(seq, dtype=F32)[:, None] * inv[None, :]
    cos, sin = jnp.cos(ang), jnp.sin(ang)
    rest = HEAD_DIM - ROT_DIM
    ones = jnp.ones((seq, rest), F32)
    zeros = jnp.zeros((seq, rest), F32)
    zh = jnp.zeros((seq, half), F32)
    ra = jnp.concatenate([cos, cos, ones], axis=-1)
    rb = jnp.concatenate([-sin, zh, zeros], axis=-1)
    rc = jnp.concatenate([zh, sin, zeros], axis=-1)
    rep = LANES // HEAD_DIM
    return tuple(jnp.tile(t, (1, rep)) for t in (ra, rb, rc))


def _pick(n, pref):
    t = min(n, pref)
    while n % t:
        t //= 2
    return t


def _trunk(x, wts):
    B, L, D = x.shape
    depth = wts["w_in"].shape[0]
    rope = _rope_tables(L)
    tm = _pick(L, 512)
    ssm_w = D // 4
    bp = -(-B // SUBLANES) * SUBLANES
    tt_ssm = _pick(L, max(1, SSM_TILE_ROWS // bp))
    us = jnp.zeros((L, bp * ssm_w), F32)
    for l in range(depth):
        lam_init = 0.8 - 0.6 * math.exp(-0.3 * l)
        q, k, v, us, up = _inproj(x, wts["g_mix"], wts["w_in"], l, rope, us, tm=tm)
        a_out = _attention(q, k, v, wts["lam_q1"], wts["lam_k1"], wts["lam_q2"], wts["lam_k2"],
                           wts["attn_subln"], l, lam_init, tq=_pick(L, 256), kc=_pick(L, 512))
        us2d = us.reshape(L * bp, ssm_w)
        y_f = _ssm_pass(us2d, wts["ssm"], l, 0, bp, tt=tt_ssm)
        s_out = _ssm_pass(us2d, wts["ssm"], l, 1, bp, tt=tt_ssm, yf=y_f,
                          gate=(wts["ssm_d"], wts["glu_w"], wts["glu_b"]))
        p_out = _pool(up, wts["pool_w"], wts["pool_scale"], l, tt=_pick(L, 256))
        x = _outproj(a_out, s_out.reshape(L, bp * ssm_w), p_out, x, wts["w_out"], l, tm=tm)
        x = _mlp(x.reshape(B * L, D), wts["g_mlp"], wts["w_up"], wts["w_down"], l,
                 tm=_pick(B * L, 1024), tf=_pick(wts["w_up"].shape[-1], 512)).reshape(B, L, D)
    return _final_norm(x.reshape(B * L, D), wts["g_final"], tm=_pick(B * L, 1024)).reshape(B, L, D)


def kernel(x_prompt, x_sample, g_mix, w_in, lam_q1, lam_k1, lam_q2, lam_k2, attn_subln, ssm_lam_re, ssm_lam_im, ssm_log_dt, ssm_b_re, ssm_b_im, ssm_c_re, ssm_c_im, ssm_d, glu_w, glu_b, pool_w, pool_scale, w_out, g_mlp, w_up, w_down, g_final):
    depth = w_in.shape[0]
    row = lambda a: a.astype(F32).reshape(depth, 1, a.shape[-1])
    wts = {
        "g_mix": row(g_mix), "w_in": w_in.astype(BF16),
        "lam_q1": row(lam_q1), "lam_k1": row(lam_k1), "lam_q2": row(lam_q2), "lam_k2": row(lam_k2),
        "attn_subln": row(attn_subln),
        "ssm": _ssm_prepare(ssm_lam_re, ssm_lam_im, ssm_log_dt, ssm_b_re, ssm_b_im, ssm_c_re, ssm_c_im),
        "ssm_d": row(ssm_d), "glu_w": glu_w.astype(BF16), "glu_b": row(glu_b),
        "pool_w": pool_w.astype(BF16), "pool_scale": row(pool_scale),
        "w_out": w_out.astype(BF16), "g_mlp": row(g_mlp),
        "w_up": w_up.astype(BF16), "w_down": w_down.astype(BF16),
        "g_final": g_final.astype(F32).reshape(1, -1),
    }
    return _trunk(x_prompt, wts), _trunk(x_sample, wts)
```

```python
import functools
import math

import jax
import jax.numpy as jnp
from jax import lax
from jax.experimental import pallas as pl
from jax.experimental.pallas import tpu as pltpu

N_HEADS = 8
HEAD_DIM = 64
V_DIM = 2 * HEAD_DIM
ROT_DIM = HEAD_DIM // 4
ROPE_THETA = 500000.0
SSM_GROUP = 16
SSM_STATE = 64
POOL_WINDOWS = (2, 4, 8, 16)
EPS = 1e-6
Q_SCALE = HEAD_DIM ** -0.5 * math.log2(math.e)

LANES = 128
SUBLANES = 8
SSM_TILE_ROWS = 1024
VMEM_LIMIT_BYTES = 56 * 2 ** 20
POOL_HALO = 128

F32 = jnp.float32
BF16 = jnp.bfloat16


def _params(*sem):
    return pltpu.CompilerParams(dimension_semantics=sem, vmem_limit_bytes=VMEM_LIMIT_BYTES)


def _rms(x, g):
    ms = jnp.mean(x * x, axis=-1, keepdims=True)
    return x * lax.rsqrt(ms + EPS) * g


def _inproj_kernel(x_ref, g_ref, w_ref, ra_ref, rb_ref, rc_ref, us_init_ref,
                   q_ref, k_ref, v_ref, us_ref, up_ref, *, qk_w, attn_w, ssm_w):
    del us_init_ref
    h = _rms(x_ref[...], g_ref[...]).astype(BF16)
    ra, rb, rc = ra_ref[...], rb_ref[...], rc_ref[...]

    def rope_heads(col0, out_ref, scale):
        p = jnp.dot(h, w_ref[:, col0:col0 + qk_w], preferred_element_type=F32)
        for hh in range(N_HEADS):
            blk = p[:, hh * V_DIM:(hh + 1) * V_DIM]
            r = (blk * ra + pltpu.roll(blk, LANES - ROT_DIM // 2, 1) * rb
                 + pltpu.roll(blk, ROT_DIM // 2, 1) * rc)
            out_ref[hh] = (r * scale).astype(BF16)

    rope_heads(0, q_ref, Q_SCALE)
    rope_heads(qk_w, k_ref, 1.0)
    o1 = 2 * qk_w
    pv = jnp.dot(h, w_ref[:, o1:o1 + attn_w], preferred_element_type=F32)
    for hh in range(N_HEADS):
        v_ref[hh] = pv[:, hh * V_DIM:(hh + 1) * V_DIM].astype(BF16)
    o2 = o1 + attn_w
    us_ref[...] = jnp.dot(h, w_ref[:, o2:o2 + ssm_w], preferred_element_type=F32)
    o3 = o2 + ssm_w
    up_ref[...] = jnp.dot(h, w_ref[:, o3:], preferred_element_type=F32).astype(BF16)


def _inproj(x, g, w_in, layer, rope, us_init, *, tm):
    B, L, D = x.shape
    in_w = w_in.shape[-1]
    qk_w = N_HEADS * 2 * HEAD_DIM
    attn_w = N_HEADS * V_DIM
    ssm_w = D // 4
    pool_w = in_w - 2 * qk_w - attn_w - ssm_w
    head_shape = jax.ShapeDtypeStruct((B, N_HEADS, L, V_DIM), BF16)
    head_spec = pl.BlockSpec((None, N_HEADS, tm, V_DIM), lambda b, i: (b, 0, i, 0))
    rope_spec = pl.BlockSpec((tm, LANES), lambda b, i: (i, 0))
    return pl.pallas_call(
        functools.partial(_inproj_kernel, qk_w=qk_w, attn_w=attn_w, ssm_w=ssm_w),
        grid=(B, L // tm),
        in_specs=[
            pl.BlockSpec((None, tm, D), lambda b, i: (b, i, 0)),
            pl.BlockSpec((None, 1, D), lambda b, i: (layer, 0, 0)),
            pl.BlockSpec((None, D, in_w), lambda b, i: (layer, 0, 0),
                         pipeline_mode=pl.Buffered(1)),
            rope_spec, rope_spec, rope_spec,
            pl.BlockSpec(memory_space=pl.ANY),
        ],
        out_specs=[
            head_spec, head_spec, head_spec,
            pl.BlockSpec((tm, ssm_w), lambda b, i: (i, b)),
            pl.BlockSpec((None, tm, pool_w), lambda b, i: (b, i, 0)),
        ],
        out_shape=[
            head_shape, head_shape, head_shape,
            jax.ShapeDtypeStruct(us_init.shape, F32),
            jax.ShapeDtypeStruct((B, L, pool_w), BF16),
        ],
        input_output_aliases={6: 3},
        compiler_params=_params("parallel", "arbitrary"),
        name="inproj",
    )(x, g, w_in, *rope, us_init)


def _attn_kernel(q_ref, k_ref, v_ref, lq1_ref, lk1_ref, lq2_ref, lk2_ref, g_ref, o_ref,
                 s0_ref, s1_ref, m0_ref, m1_ref, l_ref, acc_ref, *, seq, tq, kc, lam_init):
    nk = seq // kc
    nl = kc // LANES
    nq = seq // tq
    s_refs, m_refs = (s0_ref, s1_ref), (m0_ref, m1_ref)
    lane = lax.broadcasted_iota(jnp.int32, (1, V_DIM), 1)
    first = lane < HEAD_DIM
    lam = (jnp.exp(jnp.sum(lq1_ref[...] * lk1_ref[...], axis=-1, keepdims=True))
           - jnp.exp(jnp.sum(lq2_ref[...] * lk2_ref[...], axis=-1, keepdims=True))
           + lam_init)
    g = g_ref[...]

    def stage(a_tile, a_slot, b_tile, b_slot):
        if a_tile is not None:
            s_dst, m_dst = s_refs[a_slot], m_refs[a_slot]
            q = q_ref[pl.ds(pl.multiple_of(a_tile * tq, tq), tq), :]
            zero = jnp.zeros_like(q)
            qs = (jnp.where(first, q, zero),
                  jnp.where(first, zero, q))
            m_dst[...] = jnp.full(m_dst.shape, -jnp.inf, F32)
        if b_tile is not None:
            s_src, m_src = s_refs[b_slot], m_refs[b_slot]
            l_ref[...] = jnp.zeros(l_ref.shape, F32)
            acc_ref[...] = jnp.zeros(acc_ref.shape, F32)
        for c in range(nk):
            if a_tile is not None:
                kj = k_ref[c * kc:(c + 1) * kc, :]
                for comp in range(2):
                    s = lax.dot_general(qs[comp], kj, (((1,), (1,)), ((), ())),
                                        preferred_element_type=F32)
                    s_dst[comp, c] = s
                    mt = m_dst[comp]
                    for t in range(nl):
                        mt = jnp.maximum(mt, s[:, t * LANES:(t + 1) * LANES])
                    m_dst[comp] = mt
            if b_tile is not None:
                vj = v_ref[c * kc:(c + 1) * kc, :]
                for comp in range(2):
                    mb = m_src[comp]
                    lt = l_ref[comp]
                    parts = []
                    for t in range(nl):
                        e = jnp.exp2(s_src[comp, c, :, t * LANES:(t + 1) * LANES] - mb)
                        lt = lt + e
                        parts.append(e.astype(BF16))
                    l_ref[comp] = lt
                    acc_ref[comp] += jnp.dot(jnp.concatenate(parts, axis=1), vj,
                                             preferred_element_type=F32)
        if a_tile is not None:
            for comp in range(2):
                m_dst[comp] = jnp.broadcast_to(jnp.max(m_dst[comp], axis=-1, keepdims=True),
                                               (tq, LANES))
        if b_tile is not None:
            l1 = jnp.sum(l_ref[0], axis=-1, keepdims=True)
            l2 = jnp.sum(l_ref[1], axis=-1, keepdims=True)
            o = acc_ref[0] / l1 - lam * (acc_ref[1] / l2)
            r0 = pl.multiple_of(b_tile * tq, tq)
            o_ref[pl.ds(r0, tq), :] = (_rms(o, g) * (1.0 - lam_init)).astype(o_ref.dtype)

    stage(0, 0, None, None)

    def pair_body(j, carry):
        stage(2 * j + 1, 1, 2 * j, 0)
        stage(2 * j + 2, 0, 2 * j + 1, 1)
        return carry

    lax.fori_loop(0, nq // 2 - 1, pair_body, 0)
    stage(nq - 1, 1, nq - 2, 0)
    stage(None, None, nq - 1, 1)


def _attention(q, k, v, lq1, lk1, lq2, lk2, subln, layer, lam_init, *, tq, kc):
    B, H, L, _ = q.shape
    assert L % (2 * tq) == 0 and L % kc == 0, "query tiles are pipelined in pairs"
    head_spec = pl.BlockSpec((None, None, L, V_DIM), lambda b, h: (b, h, 0, 0))
    lam_spec = pl.BlockSpec((None, 1, HEAD_DIM), lambda b, h: (layer, 0, 0))
    return pl.pallas_call(
        functools.partial(_attn_kernel, seq=L, tq=tq, kc=kc, lam_init=lam_init),
        grid=(B, H),
        in_specs=[head_spec, head_spec, head_spec, lam_spec, lam_spec, lam_spec, lam_spec,
                  pl.BlockSpec((None, 1, V_DIM), lambda b, h: (layer, 0, 0))],
        out_specs=pl.BlockSpec((None, L, V_DIM), lambda b, h: (b, 0, h)),
        out_shape=jax.ShapeDtypeStruct((B, L, H * V_DIM), BF16),
        scratch_shapes=[pltpu.VMEM((2, L // kc, tq, kc), F32),
                        pltpu.VMEM((2, L // kc, tq, kc), F32),
                        pltpu.VMEM((2, tq, LANES), F32),
                        pltpu.VMEM((2, tq, LANES), F32),
                        pltpu.VMEM((2, tq, LANES), F32),
                        pltpu.VMEM((2, tq, V_DIM), F32)],
        compiler_params=_params("parallel", "arbitrary"),
        name="diff_attention",
    )(q, k, v, lq1, lk1, lq2, lk2, subln)


def _ssm_kernel(*refs, batch, tt, reverse, final, nblk, blk_state):
    if final:
        (u_ref, bb_ref, cc_ref, ar_ref, ai_ref, yf_ref, d_ref, gw_ref, gb_ref,
         y_ref, vs_ref, hr_ref, hi_ref, z_ref) = refs
    else:
        u_ref, bb_ref, cc_ref, ar_ref, ai_ref, y_ref, vs_ref, hr_ref, hi_ref = refs

    @pl.when(pl.program_id(0) == 0)
    def _():
        hr_ref[...] = jnp.zeros_like(hr_ref)
        hi_ref[...] = jnp.zeros_like(hi_ref)

    for qb in range(nblk):
        cols = slice(qb * LANES, (qb + 1) * LANES)
        scols = slice(qb * blk_state, (qb + 1) * blk_state)
        ub = u_ref[:, cols].astype(BF16)
        vq_ref = vs_ref.at[qb]
        vq_ref[...] = jnp.dot(ub, bb_ref[qb], preferred_element_type=F32)
        ar = jnp.broadcast_to(ar_ref[:, scols], (batch, blk_state))
        ai = jnp.broadcast_to(ai_ref[:, scols], (batch, blk_state))

        hr, hi = hr_ref[qb], hi_ref[qb]
        for s in range(tt):
            r0 = ((tt - 1 - s) if reverse else s) * batch
            vr = vq_ref[r0:r0 + batch, 0:blk_state]
            vi = vq_ref[r0:r0 + batch, blk_state:2 * blk_state]
            hr, hi = ar * hr - ai * hi + vr, ar * hi + ai * hr + vi
            vq_ref[r0:r0 + batch, 0:blk_state] = hr
            vq_ref[r0:r0 + batch, blk_state:2 * blk_state] = hi
        hr_ref[qb] = hr
        hi_ref[qb] = hi
        yq = jnp.dot(vq_ref[...].astype(BF16), cc_ref[qb], preferred_element_type=F32)
        if final:
            yt = yq + yf_ref[:, cols] + d_ref[:, cols] * u_ref[:, cols]
            z_ref[:, cols] = jax.nn.gelu(yt)
        else:
            y_ref[:, cols] = yq

    if final:
        z = z_ref[...]
        gate = jax.nn.sigmoid(jnp.dot(z.astype(BF16), gw_ref[...], preferred_element_type=F32)
                              + gb_ref[...])
        y_ref[...] = (z * gate).astype(y_ref.dtype)


def _ssm_pass(u2d, prep, layer, direction, batch, *, tt, yf=None, gate=None):
    rows, width = u2d.shape
    bb, cc, ar, ai = prep
    nblk = width // LANES
    blk_state = (LANES // SSM_GROUP) * SSM_STATE
    tile = tt * batch
    nt = rows // tile
    reverse = direction == 1
    final = yf is not None
    tile_idx = (lambda j: (nt - 1 - j, 0)) if reverse else (lambda j: (j, 0))
    row_spec = pl.BlockSpec((tile, width), tile_idx)
    dir_idx = lambda j: (layer, direction, 0, 0, 0)
    coef_spec = pl.BlockSpec((None, None, 1, nblk * blk_state), lambda j: (layer, direction, 0, 0))
    in_specs = [
        row_spec,
        pl.BlockSpec((None, None, nblk, LANES, 2 * blk_state), dir_idx),
        pl.BlockSpec((None, None, nblk, 2 * blk_state, LANES), dir_idx),
        coef_spec, coef_spec,
    ]
    args = [u2d, bb, cc, ar, ai]
    scratch = [pltpu.VMEM((nblk, tile, 2 * blk_state), F32),
               pltpu.VMEM((nblk, batch, blk_state), F32),
               pltpu.VMEM((nblk, batch, blk_state), F32)]
    if final:
        d, gw, gb = gate
        vec_spec = pl.BlockSpec((None, 1, width), lambda j: (layer, 0, 0))
        in_specs += [row_spec, vec_spec,
                     pl.BlockSpec((None, width, width), lambda j: (layer, 0, 0)), vec_spec]
        args += [yf, d, gw, gb]
        scratch.append(pltpu.VMEM((tile, width), F32))
    return pl.pallas_call(
        functools.partial(_ssm_kernel, batch=batch, tt=tt, reverse=reverse, final=final,
                          nblk=nblk, blk_state=blk_state),
        grid=(nt,),
        in_specs=in_specs,
        out_specs=row_spec,
        out_shape=jax.ShapeDtypeStruct((rows, width), BF16 if final else F32),
        scratch_shapes=scratch,
        compiler_params=_params("arbitrary"),
        name="ssm_bwd_gate" if final else "ssm_fwd",
    )(*args)


def _ssm_prepare(lam_re, lam_im, log_dt, b_re, b_im, c_re, c_im):
    depth, _, G, P = lam_re.shape
    Hc = b_re.shape[-1]
    gpb = LANES // Hc
    nblk = G // gpb
    lr = lam_re.astype(F32)
    li = lam_im.astype(F32)
    dt = jnp.exp(log_dt.astype(F32))[..., None]
    mag = jnp.exp(lr * dt)
    ar = mag * jnp.cos(li * dt)
    ai = mag * jnp.sin(li * dt)
    den = lr * lr + li * li
    nr = ar - 1.0
    cr = (nr * lr + ai * li) / den
    ci = (ai * lr - nr * li) / den
    br = b_re.astype(F32)
    bi = b_im.astype(F32)
    bbr = cr[..., None] * br - ci[..., None] * bi
    bbi = cr[..., None] * bi + ci[..., None] * br
    eye = jnp.eye(gpb, dtype=F32)

    def pack_b(w):
        w = w.reshape(depth, 2, nblk, gpb, P, Hc)
        return jnp.einsum('dzqgpj,gh->dzqgjhp', w, eye).reshape(depth, 2, nblk, gpb * Hc, gpb * P)

    def pack_c(w):
        w = w.reshape(depth, 2, nblk, gpb, Hc, P)
        return jnp.einsum('dzqgip,gh->dzqgphi', w, eye).reshape(depth, 2, nblk, gpb * P, gpb * Hc)

    bb = jnp.concatenate([pack_b(bbr), pack_b(bbi)], axis=-1).astype(BF16)
    cc = jnp.concatenate([pack_c(c_re.astype(F32)), -pack_c(c_im.astype(F32))], axis=-2).astype(BF16)
    return bb, cc, ar.reshape(depth, 2, 1, G * P), ai.reshape(depth, 2, 1, G * P)


def _pool_kernel(u_ref, w_ref, sc_ref, o_ref, pad_ref, *, seq, tt):
    nwin = len(POOL_WINDOWS)
    cw = u_ref.shape[-1] // nwin
    zeros = jnp.zeros((POOL_HALO, u_ref.shape[-1]), pad_ref.dtype)
    pad_ref[0:POOL_HALO, :] = zeros
    pad_ref[POOL_HALO + seq:POOL_HALO + seq + POOL_HALO, :] = zeros
    pad_ref[POOL_HALO:POOL_HALO + seq, :] = u_ref[...]

    kw = tt + 2 * POOL_HALO
    row = lax.broadcasted_iota(jnp.int32, (tt, kw), 0)
    col = lax.broadcasted_iota(jnp.int32, (tt, kw), 1)
    off = col - POOL_HALO - row
    bands = [jnp.where((off >= -(w // 2)) & (off <= w // 2 - 1), 1.0, 0.0).astype(BF16)
             for w in POOL_WINDOWS]
    trow = lax.broadcasted_iota(jnp.int32, (tt, 1), 0)

    def tile_body(i, carry):
        t0 = pl.multiple_of(i * tt, tt)
        t = trow + t0
        win = pad_ref[pl.ds(t0, kw), :]
        cur = pad_ref[pl.ds(t0 + POOL_HALO, tt), :].astype(F32)
        for gi, w in enumerate(POOL_WINDOWS):
            cs = slice(gi * cw, (gi + 1) * cw)
            end = jnp.minimum(t + w // 2, seq)
            start = jnp.maximum(t - w // 2, 0)
            cnt = (end - start).astype(F32)
            ssum = jnp.dot(bands[gi], win[:, cs], preferred_element_type=F32)
            pooled = ssum / cnt - cur[:, cs]
            out = jnp.dot(pooled.astype(BF16), w_ref[gi], preferred_element_type=F32)
            o_ref[pl.ds(t0, tt), cs] = (out * sc_ref[:, cs]).astype(o_ref.dtype)
        return carry

    lax.fori_loop(0, seq // tt, tile_body, 0)


def _pool(u, pool_w, pool_scale, layer, *, tt):
    B, L, W = u.shape
    nwin = len(POOL_WINDOWS)
    cw = W // nwin
    seq_spec = pl.BlockSpec((None, L, W), lambda b: (b, 0, 0))
    return pl.pallas_call(
        functools.partial(_pool_kernel, seq=L, tt=tt),
        grid=(B,),
        in_specs=[seq_spec,
                  pl.BlockSpec((None, nwin, cw, cw), lambda b: (layer, 0, 0, 0)),
                  pl.BlockSpec((None, 1, W), lambda b: (layer, 0, 0))],
        out_specs=seq_spec,
        out_shape=jax.ShapeDtypeStruct((B, L, W), BF16),
        scratch_shapes=[pltpu.VMEM((L + 2 * POOL_HALO, W), BF16)],
        compiler_params=_params("parallel"),
        name="pool_mixer",
    )(u, pool_w, pool_scale)


def _outproj_kernel(a_ref, s_ref, p_ref, x_ref, w_ref, o_ref):
    aw = a_ref.shape[-1]
    sw = s_ref.shape[-1]
    acc = jnp.dot(a_ref[...], w_ref[0:aw, :], preferred_element_type=F32)
    acc += jnp.dot(s_ref[...], w_ref[aw:aw + sw, :], preferred_element_type=F32)
    acc += jnp.dot(p_ref[...], w_ref[aw + sw:, :], preferred_element_type=F32)
    o_ref[...] = x_ref[...] + acc


def _outproj(a, s, p, x, w_out, layer, *, tm):
    B, L, D = x.shape
    aw, pw = a.shape[-1], p.shape[-1]
    sw = w_out.shape[1] - aw - pw
    return pl.pallas_call(
        _outproj_kernel,
        grid=(B, L // tm),
        in_specs=[
            pl.BlockSpec((None, tm, aw), lambda b, i: (b, i, 0)),
            pl.BlockSpec((tm, sw), lambda b, i: (i, b)),
            pl.BlockSpec((None, tm, pw), lambda b, i: (b, i, 0)),
            pl.BlockSpec((None, tm, D), lambda b, i: (b, i, 0)),
            pl.BlockSpec((None, aw + sw + pw, D), lambda b, i: (layer, 0, 0),
                         pipeline_mode=pl.Buffered(1)),
        ],
        out_specs=pl.BlockSpec((None, tm, D), lambda b, i: (b, i, 0)),
        out_shape=jax.ShapeDtypeStruct((B, L, D), F32),
        input_output_aliases={3: 0},
        compiler_params=_params("parallel", "arbitrary"),
        name="outproj_residual",
    )(a, s, p, x, w_out)


def _mlp_kernel(x_ref, g_ref, wu_ref, wd_ref, o_ref, h_ref):
    @pl.when(pl.program_id(1) == 0)
    def _():
        xv = x_ref[...]
        h_ref[...] = _rms(xv, g_ref[...]).astype(BF16)
        o_ref[...] = xv

    u = jnp.dot(h_ref[...], wu_ref[...], preferred_element_type=F32)
    r = jnp.maximum(u, 0.0)
    o_ref[...] += jnp.dot((r * r).astype(BF16), wd_ref[...], preferred_element_type=F32)


def _mlp(x2d, g, w_up, w_down, layer, *, tm, tf):
    T, D = x2d.shape
    F = w_up.shape[-1]
    return pl.pallas_call(
        _mlp_kernel,
        grid=(T // tm, F // tf),
        in_specs=[
            pl.BlockSpec((tm, D), lambda i, j: (i, 0)),
            pl.BlockSpec((None, 1, D), lambda i, j: (layer, 0, 0)),
            pl.BlockSpec((None, D, tf), lambda i, j: (layer, 0, j)),
            pl.BlockSpec((None, tf, D), lambda i, j: (layer, j, 0)),
        ],
        out_specs=pl.BlockSpec((tm, D), lambda i, j: (i, 0)),
        out_shape=jax.ShapeDtypeStruct((T, D), F32),
        scratch_shapes=[pltpu.VMEM((tm, D), BF16)],
        input_output_aliases={0: 0},
        compiler_params=_params("parallel", "arbitrary"),
        name="mlp_residual",
    )(x2d, g, w_up, w_down)


def _final_norm_kernel(x_ref, g_ref, o_ref):
    o_ref[...] = _rms(x_ref[...], g_ref[...])


def _final_norm(x2d, g, *, tm):
    T, D = x2d.shape
    return pl.pallas_call(
        _final_norm_kernel,
        grid=(T // tm,),
        in_specs=[pl.BlockSpec((tm, D), lambda i: (i, 0)), pl.BlockSpec((1, D), lambda i: (0, 0))],
        out_specs=pl.BlockSpec((tm, D), lambda i: (i, 0)),
        out_shape=jax.ShapeDtypeStruct((T, D), F32),
        compiler_params=_params("parallel"),
        name="final_norm",
    )(x2d, g)


def _rope_tables(seq):
    half = ROT_DIM // 2
    inv = ROPE_THETA ** (-jnp.arange(0, ROT_DIM, 2, dtype=F32) / ROT_DIM)
    ang = jnp.arange(seq, dtype=F32)[:, None] * inv[None, :]
    cos, sin = jnp.cos(ang), jnp.sin(ang)
    rest = HEAD_DIM - ROT_DIM
    ones = jnp.ones((seq, rest), F32)
    zeros = jnp.zeros((seq, rest), F32)
    zh = jnp.zeros((seq, half), F32)
    ra = jnp.concatenate([cos, cos, ones], axis=-1)
    rb = jnp.concatenate([-sin, zh, zeros], axis=-1)
    rc = jnp.concatenate([zh, sin, zeros], axis=-1)
    rep = LANES // HEAD_DIM
    return tuple(jnp.tile(t, (1, rep)) for t in (ra, rb, rc))


def _pick(n, pref):
    t = min(n, pref)
    while n % t:
        t //= 2
    return t


def _trunk(x, wts):
    B, L, D = x.shape
    depth = wts["w_in"].shape[0]
    rope = _rope_tables(L)
    tm = _pick(L, 512)
    ssm_w = D // 4
    bp = -(-B // SUBLANES) * SUBLANES
    tt_ssm = _pick(L, max(1, SSM_TILE_ROWS // bp))
    us = jnp.zeros((L, bp * ssm_w), F32)
    for l in range(depth):
        lam_init = 0.8 - 0.6 * math.exp(-0.3 * l)
        q, k, v, us, up = _inproj(x, wts["g_mix"], wts["w_in"], l, rope, us, tm=tm)
        a_out = _attention(q, k, v, wts["lam_q1"], wts["lam_k1"], wts["lam_q2"], wts["lam_k2"],
                           wts["attn_subln"], l, lam_init, tq=_pick(L // 2, 256), kc=_pick(L, 512))
        us2d = us.reshape(L * bp, ssm_w)
        y_f = _ssm_pass(us2d, wts["ssm"], l, 0, bp, tt=tt_ssm)
        s_out = _ssm_pass(us2d, wts["ssm"], l, 1, bp, tt=tt_ssm, yf=y_f,
                          gate=(wts["ssm_d"], wts["glu_w"], wts["glu_b"]))
        p_out = _pool(up, wts["pool_w"], wts["pool_scale"], l, tt=_pick(L, 256))
        x = _outproj(a_out, s_out.reshape(L, bp * ssm_w), p_out, x, wts["w_out"], l, tm=tm)
        x = _mlp(x.reshape(B * L, D), wts["g_mlp"], wts["w_up"], wts["w_down"], l,
                 tm=_pick(B * L, 1024), tf=_pick(wts["w_up"].shape[-1], 512)).reshape(B, L, D)
    return _final_norm(x.reshape(B * L, D), wts["g_final"], tm=_pick(B * L, 1024)).reshape(B, L, D)


def kernel(x_prompt, x_sample, g_mix, w_in, lam_q1, lam_k1, lam_q2, lam_k2, attn_subln, ssm_lam_re, ssm_lam_im, ssm_log_dt, ssm_b_re, ssm_b_im, ssm_c_re, ssm_c_im, ssm_d, glu_w, glu_b, pool_w, pool_scale, w_out, g_mlp, w_up, w_down, g_final):
    depth = w_in.shape[0]
    row = lambda a: a.astype(F32).reshape(depth, 1, a.shape[-1])
    wts = {
        "g_mix": row(g_mix), "w_in": w_in.astype(BF16),
        "lam_q1": row(lam_q1), "lam_k1": row(lam_k1), "lam_q2": row(lam_q2), "lam_k2": row(lam_k2),
        "attn_subln": row(attn_subln),
        "ssm": _ssm_prepare(ssm_lam_re, ssm_lam_im, ssm_log_dt, ssm_b_re, ssm_b_im, ssm_c_re, ssm_c_im),
        "ssm_d": row(ssm_d), "glu_w": glu_w.astype(BF16), "glu_b": row(glu_b),
        "pool_w": pool_w.astype(BF16), "pool_scale": row(pool_scale),
        "w_out": w_out.astype(BF16), "g_mlp": row(g_mlp),
        "w_up": w_up.astype(BF16), "w_down": w_down.astype(BF16),
        "g_final": g_final.astype(F32).reshape(1, -1),
    }
    return _trunk(x_prompt, wts), _trunk(x_sample, wts)
```
